```python
import math
import jax, jax.numpy as jnp
from jax import lax
import numpy as np

D_MODEL = 2048
BATCH = 4
SEQ = 4096
DEPTH = 1

EPS = 1e-6
MLA_HEADS = 8
MLA_NOPE = 128
MLA_ROPE = 64
MLA_QK = MLA_NOPE + MLA_ROPE
MLA_V = 128
MLA_Q_RANK = 512
MLA_KV_RANK = 256
MLA_WIDTH = MLA_HEADS * MLA_V
ROPE_THETA = 10000.0
Q_BLOCK = 128
GLA_HEADS = 4
GLA_DK = 128
GLA_DV = 256
GLA_WIDTH = GLA_HEADS * GLA_DV
GLA_GATE_RANK = 16
GLA_TAU = 16.0
GLA_CHUNK = 64
PEER_HEADS = 8
PEER_NKEYS = 128
PEER_EXPERTS = PEER_NKEYS * PEER_NKEYS
PEER_QDIM = 256
PEER_HALF = PEER_QDIM // 2
PEER_TOPK = 16
PEER_TOKEN_BLOCK = 128
IN_SPLITS = (MLA_Q_RANK, MLA_KV_RANK, MLA_ROPE, GLA_HEADS * GLA_DK, GLA_HEADS * GLA_DK,
             GLA_WIDTH, GLA_GATE_RANK, GLA_WIDTH)
IN_WIDTH = MLA_Q_RANK + MLA_KV_RANK + MLA_ROPE + 2 * GLA_HEADS * GLA_DK + 2 * GLA_WIDTH + GLA_GATE_RANK
MIX_WIDTH = MLA_WIDTH + GLA_WIDTH

kernel_name = "hybrid_mla_gla_peer_adaln"


def rmsnorm(x, g):
    xf = x.astype(jnp.float32)
    y = xf * lax.rsqrt(jnp.mean(xf * xf, axis=-1, keepdims=True) + EPS)
    return (y * g.astype(jnp.float32)).astype(x.dtype)


def apply_rope(x, cos, sin):
    half = x.shape[-1] // 2
    x1, x2 = x[..., :half], x[..., half:]
    return jnp.concatenate([x1 * cos - x2 * sin, x2 * cos + x1 * sin], axis=-1)


def causal_block_attention(q, k, v):
    B, S, H, Dq = q.shape
    Dv = v.shape[-1]
    nb = S // Q_BLOCK
    qb = q.reshape(B, nb, Q_BLOCK, H, Dq).transpose(1, 0, 2, 3, 4)
    kpos = jnp.arange(S)
    scale = Dq ** -0.5

    def one_block(args):
        qi, bi = args
        s = jnp.einsum('bqhd,bkhd->bhqk', qi, k).astype(jnp.float32) * scale
        qpos = bi * Q_BLOCK + jnp.arange(Q_BLOCK)
        s = jnp.where(kpos[None, :] <= qpos[:, None], s, -jnp.inf)
        p = jax.nn.softmax(s, axis=-1).astype(v.dtype)
        return jnp.einsum('bhqk,bkhv->bqhv', p, v)

    out = lax.map(one_block, (qb, jnp.arange(nb)))
    return out.transpose(1, 0, 2, 3, 4).reshape(B, S, H, Dv)


def gla_chunked(q, k, v, log_a):
    B, S, H, dk = q.shape
    dv = v.shape[-1]
    C = GLA_CHUNK
    n = S // C

    def to_chunks(t):
        return t.reshape(B, n, C, H, t.shape[-1]).transpose(0, 3, 1, 2, 4)

    q, k, v, la = to_chunks(q), to_chunks(k), to_chunks(v), to_chunks(log_a)
    bcum = lax.cumsum(la, axis=3)
    b_last = bcum[:, :, :, -1:, :]
    b_mid = bcum[:, :, :, C // 2 - 1:C // 2, :]
    qe = q * jnp.exp(bcum - b_mid)
    ke = k * jnp.exp(b_mid - bcum)
    A = jnp.einsum('bhnid,bhnjd->bhnij', qe, ke)
    causal = jnp.tril(jnp.ones((C, C), dtype=bool))
    A = jnp.where(causal, A, 0.0)
    o_intra = jnp.einsum('bhnij,bhnjv->bhniv', A, v)
    U = jnp.einsum('bhnjd,bhnjv->bhndv', k * jnp.exp(b_last - bcum), v)
    decay = jnp.exp(b_last[:, :, :, 0, :])

    def step(state, inp):
        dec, u = inp
        return dec[..., None] * state + u, state

    S0 = jnp.zeros((B, H, dk, dv), q.dtype)
    _, S_before = lax.scan(step, S0, (decay.transpose(2, 0, 1, 3), U.transpose(2, 0, 1, 3, 4)))
    S_before = S_before.transpose(1, 2, 0, 3, 4)
    o_inter = jnp.einsum('bhnid,bhndv->bhniv', q * jnp.exp(bcum), S_before)
    o = o_intra + o_inter
    return o.transpose(0, 2, 3, 1, 4).reshape(B, S, H, dv)


def hybrid_mixer(h, cos, sin, w_in, q_norm_g, w_uq, kv_norm_g, w_ukv, mla_out_g,
                 gla_w_gate2, gla_b_gate, gla_out_g, w_out):
    B, S, _ = h.shape
    proj = h @ w_in
    offsets = [int(o) for o in np.cumsum(IN_SPLITS)[:-1]]
    cq, ckv, k_pe, gq, gk, gv, glr, gr = jnp.split(proj, offsets, axis=-1)

    q = (rmsnorm(cq, q_norm_g) @ w_uq).reshape(B, S, MLA_HEADS, MLA_QK)
    q_pe = apply_rope(q[..., MLA_NOPE:], cos[:, :, None, :], sin[:, :, None, :])
    q = jnp.concatenate([q[..., :MLA_NOPE], q_pe], axis=-1)
    kv = (rmsnorm(ckv, kv_norm_g) @ w_ukv).reshape(B, S, MLA_HEADS, MLA_NOPE + MLA_V)
    k_nope, v = kv[..., :MLA_NOPE], kv[..., MLA_NOPE:]
    k_pe = apply_rope(k_pe, cos, sin)
    k = jnp.concatenate([k_nope, jnp.broadcast_to(k_pe[:, :, None, :], (B, S, MLA_HEADS, MLA_ROPE))], axis=-1)
    o_mla = causal_block_attention(q, k, v)
    o_mla = rmsnorm(o_mla, mla_out_g.reshape(MLA_HEADS, MLA_V)).reshape(B, S, MLA_WIDTH)

    gq = gq.reshape(B, S, GLA_HEADS, GLA_DK).astype(jnp.float32) * (GLA_DK ** -0.5)
    gk = gk.reshape(B, S, GLA_HEADS, GLA_DK).astype(jnp.float32)
    gv = gv.reshape(B, S, GLA_HEADS, GLA_DV).astype(jnp.float32)
    log_a = jax.nn.log_sigmoid((glr @ gla_w_gate2 + gla_b_gate).astype(jnp.float32)) / GLA_TAU
    log_a = log_a.reshape(B, S, GLA_HEADS, GLA_DK)
    o_gla = gla_chunked(gq, gk, gv, log_a).astype(h.dtype)
    o_gla = rmsnorm(o_gla, gla_out_g.reshape(GLA_HEADS, GLA_DV)).reshape(B, S, GLA_WIDTH) * jax.nn.silu(gr)

    return jnp.concatenate([o_mla, o_gla], axis=-1) @ w_out


def peer_ffn(h, w_q, sub_keys, u_tab, v_tab):
    B, S, D = h.shape
    q = (h @ w_q).reshape(B, S, PEER_HEADS, 2, PEER_HALF)
    s = jnp.einsum('bshpd,hpkd->bshpk', q, sub_keys).astype(jnp.float32)
    top_s, top_i = lax.top_k(s, PEER_TOPK)
    cand = top_s[..., 0, :, None] + top_s[..., 1, None, :]
    cand = cand.reshape(B, S, PEER_HEADS, PEER_TOPK * PEER_TOPK)
    best_s, best_c = lax.top_k(cand, PEER_TOPK)
    i1 = jnp.take_along_axis(top_i[..., 0, :], best_c // PEER_TOPK, axis=-1)
    i2 = jnp.take_along_axis(top_i[..., 1, :], best_c % PEER_TOPK, axis=-1)
    expert = i1 * PEER_NKEYS + i2
    gate = jax.nn.softmax(best_s, axis=-1).astype(h.dtype)
    E = PEER_HEADS * PEER_TOPK
    nb = (B * S) // PEER_TOKEN_BLOCK
    xb = h.reshape(nb, PEER_TOKEN_BLOCK, D)
    eb = expert.reshape(nb, PEER_TOKEN_BLOCK, E)
    gb = gate.reshape(nb, PEER_TOKEN_BLOCK, E)

    def block(args):
        xt, et, gt = args
        u = jnp.take(u_tab, et, axis=0)
        act = jax.nn.gelu(jnp.einsum('td,ted->te', xt, u), approximate=False)
        return jnp.einsum('te,ted->td', gt * act, jnp.take(v_tab, et, axis=0))

    out = lax.map(block, (xb, eb, gb))
    return out.reshape(B, S, D)


def setup_inputs(seed: int = 0) -> dict:
    key = jax.random.key(seed)
    ks = jax.random.split(key, 24)
    L, D = DEPTH, D_MODEL
    f32 = jnp.float32

    def nrm(k, shape, scale):
        return jax.random.normal(k, shape, f32) * scale

    def gain(k, shape):
        return 1.0 + 0.02 * jax.random.normal(k, shape, f32)

    return {
        "x": nrm(ks[0], (BATCH, SEQ, D), 1.0),
        "c": nrm(ks[1], (BATCH, D), 1.0),
        "positions": jnp.broadcast_to(jnp.arange(SEQ, dtype=jnp.int32)[None, :], (BATCH, SEQ)),
        "ada_w": nrm(ks[2], (L, D, 6 * D), 0.5 * D ** -0.5),
        "ada_b": nrm(ks[3], (L, 6 * D), 0.02),
        "mix_norm_g": gain(ks[4], (L, D)),
        "w_in": nrm(ks[5], (L, D, IN_WIDTH), D ** -0.5),
        "mla_q_norm_g": gain(ks[6], (L, MLA_Q_RANK)),
        "mla_w_uq": nrm(ks[7], (L, MLA_Q_RANK, MLA_HEADS * MLA_QK), MLA_Q_RANK ** -0.5),
        "mla_kv_norm_g": gain(ks[8], (L, MLA_KV_RANK)),
        "mla_w_ukv": nrm(ks[9], (L, MLA_KV_RANK, MLA_HEADS * (MLA_NOPE + MLA_V)), MLA_KV_RANK ** -0.5),
        "mla_out_norm_g": gain(ks[10], (L, MLA_WIDTH)),
        "gla_w_gate2": nrm(ks[11], (L, GLA_GATE_RANK, GLA_HEADS * GLA_DK), GLA_GATE_RANK ** -0.5),
        "gla_b_gate": nrm(ks[12], (L, GLA_HEADS * GLA_DK), 0.1),
        "gla_out_norm_g": gain(ks[13], (L, GLA_WIDTH)),
        "w_out": nrm(ks[14], (L, MIX_WIDTH, D), MIX_WIDTH ** -0.5),
        "ffn_norm_g": gain(ks[15], (L, D)),
        "peer_w_q": nrm(ks[16], (L, D, PEER_HEADS * PEER_QDIM), D ** -0.5),
        "peer_sub_keys": nrm(ks[17], (L, PEER_HEADS, 2, PEER_NKEYS, PEER_HALF), PEER_HALF ** -0.5),
        "peer_u": nrm(ks[18], (L, PEER_EXPERTS, D), D ** -0.5),
        "peer_v": nrm(ks[19], (L, PEER_EXPERTS, D), 1.0),
        "final_norm_g": gain(ks[20], (D,)),
    }


def reference(x, c, positions, ada_w, ada_b, mix_norm_g, w_in, mla_q_norm_g, mla_w_uq,
              mla_kv_norm_g, mla_w_ukv, mla_out_norm_g, gla_w_gate2, gla_b_gate,
              gla_out_norm_g, w_out, ffn_norm_g, peer_w_q, peer_sub_keys, peer_u, peer_v,
              final_norm_g):
    inv_freq = ROPE_THETA ** (-jnp.arange(0, MLA_ROPE, 2, dtype=jnp.float32) / MLA_ROPE)
    ang = positions.astype(jnp.float32)[..., None] * inv_freq
    cos = jnp.cos(ang).astype(x.dtype)
    sin = jnp.sin(ang).astype(x.dtype)
    c_act = jax.nn.silu(c)
    for l in range(DEPTH):
        mod = (c_act @ ada_w[l] + ada_b[l])[:, None, :]
        sh1, sc1, g1, sh2, sc2, g2 = jnp.split(mod, 6, axis=-1)
        h = rmsnorm(x, mix_norm_g[l]) * (1.0 + sc1) + sh1
        x = x + g1 * hybrid_mixer(h, cos, sin, w_in[l], mla_q_norm_g[l], mla_w_uq[l],
                                  mla_kv_norm_g[l], mla_w_ukv[l], mla_out_norm_g[l],
                                  gla_w_gate2[l], gla_b_gate[l], gla_out_norm_g[l], w_out[l])
        h = rmsnorm(x, ffn_norm_g[l]) * (1.0 + sc2) + sh2
        x = x + g2 * peer_ffn(h, peer_w_q[l], peer_sub_keys[l], peer_u[l], peer_v[l])
    return rmsnorm(x, final_norm_g)
```

```python
import functools
import math

import jax
import jax.numpy as jnp
from jax import lax
from jax.experimental import pallas as pl
from jax.experimental.pallas import tpu as pltpu

F32 = jnp.float32
BF16 = jnp.bfloat16

EPS = 1e-6
MLA_HEADS = 8
MLA_NOPE = 128
MLA_ROPE = 64
MLA_QK = MLA_NOPE + MLA_ROPE
MLA_V = 128
MLA_Q_RANK = 512
MLA_KV_RANK = 256
ROPE_THETA = 10000.0
GLA_HEADS = 4
GLA_DK = 128
GLA_DV = 256
GLA_GATE_RANK = 16
GLA_TAU = 16.0
GLA_CHUNK = 64
PEER_HEADS = 8
PEER_NKEYS = 128
PEER_TOPK = 16

LANE = 128
NEG = -1e30
IN_COLS = 4608

OFF_CQ = 0
OFF_CKV = 512
OFF_GQ = 768
OFF_GK = 1280
OFF_GV = 1792
OFF_GR = 2816
OFF_KPE = 3840
OFF_GLR = 4096


def _params(sem, vmem_mb):
    return pltpu.CompilerParams(dimension_semantics=sem, vmem_limit_bytes=vmem_mb * 1024 * 1024)


def _dot(a, b):
    return jnp.dot(a, b, preferred_element_type=F32)


def _dot_nt(a, b):
    return lax.dot_general(a, b, (((1,), (1,)), ((), ())), preferred_element_type=F32)


def _dot_tn(a, b):
    return lax.dot_general(a, b, (((0,), (0,)), ((), ())), preferred_element_type=F32)


def _rms(x, g):
    return x * lax.rsqrt(jnp.mean(x * x, axis=-1, keepdims=True) + EPS) * g


def _mod_kernel(c_ref, w_ref, b_ref, o_ref):
    c = c_ref[...]
    ca = (c * jax.nn.sigmoid(c)).astype(BF16)
    o_ref[...] = _dot(ca, w_ref[...].astype(BF16)) + b_ref[...]


def _mod(c8, ada_w, ada_b):
    d, n = ada_w.shape
    tn = 512
    return pl.pallas_call(
        _mod_kernel,
        grid=(n // tn,),
        in_specs=[
            pl.BlockSpec((8, d), lambda j: (0, 0)),
            pl.BlockSpec((d, tn), lambda j: (0, j)),
            pl.BlockSpec((1, tn), lambda j: (0, j)),
        ],
        out_specs=pl.BlockSpec((8, tn), lambda j: (0, j)),
        out_shape=jax.ShapeDtypeStruct((8, n), F32),
        compiler_params=_params(("arbitrary",), 32),
        name="mod",
    )(c8, ada_w, ada_b)


def _inproj_kernel(x_ref, sc_ref, sh_ref, g_ref, w_ref, o_ref, h_scr):
    @pl.when(pl.program_id(1) == 0)
    def _():
        h = _rms(x_ref[...], g_ref[...]) * (1.0 + sc_ref[0]) + sh_ref[0]
        h_scr[...] = h.astype(BF16)

    o_ref[...] = _dot(h_scr[...], w_ref[...]).astype(o_ref.dtype)


def _inproj(x2d, sc, sh, g, w, seq):
    n, d = x2d.shape
    cols = w.shape[1]
    tm = min(1024, seq)
    tn = 768
    return pl.pallas_call(
        _inproj_kernel,
        grid=(n // tm, cols // tn),
        in_specs=[
            pl.BlockSpec((tm, d), lambda i, j: (i, 0)),
            pl.BlockSpec((1, 1, d), lambda i, j: (i * tm // seq, 0, 0)),
            pl.BlockSpec((1, 1, d), lambda i, j: (i * tm // seq, 0, 0)),
            pl.BlockSpec((1, d), lambda i, j: (0, 0)),
            pl.BlockSpec((d, tn), lambda i, j: (0, j)),
        ],
        out_specs=pl.BlockSpec((tm, tn), lambda i, j: (i, j)),
        out_shape=jax.ShapeDtypeStruct((n, cols), BF16),
        scratch_shapes=[pltpu.VMEM((tm, d), BF16)],
        compiler_params=_params(("parallel", "arbitrary"), 48),
        name="inproj",
    )(x2d, sc, sh, g, w)


def _mlaprep_kernel(cq_ref, ckv_ref, kpe_ref, pos_ref, frq_ref, gq_ref, gkv_ref,
                    wq_ref, wqr_ref, wkv_ref, q_ref, kv_ref, kpeo_ref):
    ang = pos_ref[...] * frq_ref[...]
    cs = jnp.cos(ang)
    sn = jnp.sin(ang)
    qn = _rms(cq_ref[...].astype(F32), gq_ref[...]).astype(BF16)
    qa = _dot(qn, wq_ref[...])
    qr = _dot(qn, wqr_ref[...])
    scale = MLA_QK ** -0.5
    for h in range(MLA_HEADS):
        a = h * 2 * LANE
        q_ref[:, a:a + LANE] = (qa[:, a:a + LANE] * scale).astype(BF16)
        pe = qa[:, a + LANE:a + 2 * LANE] * cs + qr[:, h * LANE:(h + 1) * LANE] * sn
        q_ref[:, a + LANE:a + 2 * LANE] = (pe * scale).astype(BF16)
    kvn = _rms(ckv_ref[...].astype(F32), gkv_ref[...]).astype(BF16)
    kv_ref[...] = _dot(kvn, wkv_ref[...]).astype(BF16)
    kp = kpe_ref[...].astype(F32)
    kpeo_ref[...] = (kp[:, :LANE] * cs + kp[:, LANE:] * sn).astype(BF16)


def _mlaprep(proj, pos, frq, gq, gkv, wq, wqr, wkv):
    n = proj.shape[0]
    tm = 512
    hq = MLA_HEADS * 2 * LANE
    const = lambda i: (0, 0)
    return pl.pallas_call(
        _mlaprep_kernel,
        grid=(n // tm,),
        in_specs=[
            pl.BlockSpec((tm, MLA_Q_RANK), lambda i: (i, OFF_CQ // MLA_Q_RANK)),
            pl.BlockSpec((tm, MLA_KV_RANK), lambda i: (i, OFF_CKV // MLA_KV_RANK)),
            pl.BlockSpec((tm, 2 * LANE), lambda i: (i, OFF_KPE // (2 * LANE))),
            pl.BlockSpec((tm, 1), lambda i: (i, 0)),
            pl.BlockSpec((1, LANE), const),
            pl.BlockSpec((1, MLA_Q_RANK), const),
            pl.BlockSpec((1, MLA_KV_RANK), const),
            pl.BlockSpec(wq.shape, const),
            pl.BlockSpec(wqr.shape, const),
            pl.BlockSpec(wkv.shape, const),
        ],
        out_specs=[
            pl.BlockSpec((tm, hq), lambda i: (i, 0)),
            pl.BlockSpec((tm, hq), lambda i: (i, 0)),
            pl.BlockSpec((tm, LANE), lambda i: (i, 0)),
        ],
        out_shape=[
            jax.ShapeDtypeStruct((n, hq), BF16),
            jax.ShapeDtypeStruct((n, hq), BF16),
            jax.ShapeDtypeStruct((n, LANE), BF16),
        ],
        compiler_params=_params(("parallel",), 48),
        name="mlaprep",
    )(proj, proj, proj, pos, frq, gq, gkv, wq, wqr, wkv)


def _attn_kernel(q_ref, kv_ref, kpe_ref, g_ref, o_ref, m_scr, l_scr, acc_scr, *, tq, tk):
    qi = pl.program_id(2)
    ki = pl.program_id(3)

    @pl.when(ki == 0)
    def _():
        m_scr[...] = jnp.full(m_scr.shape, NEG, F32)
        l_scr[...] = jnp.zeros(l_scr.shape, F32)
        acc_scr[...] = jnp.zeros(acc_scr.shape, F32)

    @pl.when(ki * tk <= qi * tq + tq - 1)
    def _():
        k = jnp.concatenate([kv_ref[:, :LANE], kpe_ref[...]], axis=1)
        v = kv_ref[:, LANE:]
        s = _dot_nt(q_ref[...], k)
        row = qi * tq + lax.broadcasted_iota(jnp.int32, s.shape, 0)
        col = ki * tk + lax.broadcasted_iota(jnp.int32, s.shape, 1)
        s = jnp.where(col <= row, s, NEG)
        m_old = m_scr[...]
        m_new = jnp.maximum(m_old, jnp.max(s, axis=-1, keepdims=True))
        alpha = jnp.exp(m_old - m_new)
        p = jnp.exp(s - m_new)
        l_scr[...] = alpha * l_scr[...] + jnp.sum(p, axis=-1, keepdims=True)
        acc_scr[...] = alpha * acc_scr[...] + _dot(p.astype(BF16), v)
        m_scr[...] = m_new

    @pl.when(ki == pl.num_programs(3) - 1)
    def _():
        o = acc_scr[...] / l_scr[...]
        o_ref[...] = _rms(o, g_ref[0]).astype(o_ref.dtype)


def _attn(q, kv, kpe, g, batch, seq):
    n = q.shape[0]
    tq = min(1024, seq)
    tk = min(512, seq)
    nq, nk = seq // tq, seq // tk

    def kblk(b, qi, ki):
        return b * nk + jnp.minimum(ki, (qi * tq + tq - 1) // tk)

    return pl.pallas_call(
        functools.partial(_attn_kernel, tq=tq, tk=tk),
        grid=(batch, MLA_HEADS, nq, nk),
        in_specs=[
            pl.BlockSpec((tq, 2 * LANE), lambda b, h, qi, ki: (b * nq + qi, h)),
            pl.BlockSpec((tk, 2 * LANE), lambda b, h, qi, ki: (kblk(b, qi, ki), h)),
            pl.BlockSpec((tk, LANE), lambda b, h, qi, ki: (kblk(b, qi, ki), 0)),
            pl.BlockSpec((1, 1, MLA_V), lambda b, h, qi, ki: (h, 0, 0)),
        ],
        out_specs=pl.BlockSpec((tq, MLA_V), lambda b, h, qi, ki: (b * nq + qi, h)),
        out_shape=jax.ShapeDtypeStruct((n, MLA_HEADS * MLA_V), BF16),
        scratch_shapes=[
            pltpu.VMEM((tq, 1), F32),
            pltpu.VMEM((tq, 1), F32),
            pltpu.VMEM((tq, MLA_V), F32),
        ],
        compiler_params=_params(("parallel", "parallel", "parallel", "arbitrary"), 48),
        name="attn",
    )(q, kv, kpe, g)


def _gla_kernel(q_ref, k_ref, v_ref, glr_ref, gr_ref, tri_ref, wg_ref, bg_ref, go_ref,
                o_ref, st_scr, *, tt):
    @pl.when(pl.program_id(2) == 0)
    def _():
        st_scr[...] = jnp.zeros(st_scr.shape, F32)

    z = _dot(glr_ref[...], wg_ref[0]) + bg_ref[0]
    la = jax.nn.log_sigmoid(z) * (1.0 / GLA_TAU)
    la_hi = la.astype(BF16)
    la_lo = (la - la_hi.astype(F32)).astype(BF16)
    tri = tri_ref[...]
    bcum = _dot(tri, la_hi) + _dot(tri, la_lo)

    c = GLA_CHUNK
    ri = lax.broadcasted_iota(jnp.int32, (c, c), 0)
    ci = lax.broadcasted_iota(jnp.int32, (c, c), 1)
    causal = ci <= ri
    st = st_scr[...]
    for n in range(tt // c):
        sl = slice(n * c, (n + 1) * c)
        b = bcum[sl]
        b_last = b[c - 1:c]
        b_mid = b[c // 2 - 1:c // 2]
        qc = q_ref[sl, :].astype(F32) * (GLA_DK ** -0.5)
        kc = k_ref[sl, :].astype(F32)
        vc = v_ref[sl, :]
        qe = (qc * jnp.exp(b - b_mid)).astype(BF16)
        ke = (kc * jnp.exp(b_mid - b)).astype(BF16)
        a = jnp.where(causal, _dot_nt(qe, ke), 0.0)
        o = _dot(a.astype(BF16), vc)
        qd = (qc * jnp.exp(b)).astype(BF16)
        o = o + _dot_nt(qd, st.astype(BF16))
        kd = (kc * jnp.exp(b_last - b)).astype(BF16)
        st = jnp.exp(b_last) * st + _dot_tn(vc, kd)
        gr = gr_ref[sl, :].astype(F32)
        o_ref[sl, :] = (_rms(o, go_ref[0]) * (gr * jax.nn.sigmoid(gr))).astype(o_ref.dtype)
    st_scr[...] = st


def _gla(proj, tri, wg, bg, go, batch, seq):
    n = proj.shape[0]
    tt = min(512, seq)
    nt = seq // tt
    row = lambda b, h, t: b * nt + t
    return pl.pallas_call(
        functools.partial(_gla_kernel, tt=tt),
        grid=(batch, GLA_HEADS, nt),
        in_specs=[
            pl.BlockSpec((tt, GLA_DK), lambda b, h, t: (row(b, h, t), OFF_GQ // GLA_DK + h)),
            pl.BlockSpec((tt, GLA_DK), lambda b, h, t: (row(b, h, t), OFF_GK // GLA_DK + h)),
            pl.BlockSpec((tt, GLA_DV), lambda b, h, t: (row(b, h, t), OFF_GV // GLA_DV + h)),
            pl.BlockSpec((tt, LANE), lambda b, h, t: (row(b, h, t), OFF_GLR // LANE)),
            pl.BlockSpec((tt, GLA_DV), lambda b, h, t: (row(b, h, t), OFF_GR // GLA_DV + h)),
            pl.BlockSpec((tt, tt), lambda b, h, t: (0, 0)),
            pl.BlockSpec((1, LANE, GLA_DK), lambda b, h, t: (h, 0, 0)),
            pl.BlockSpec((1, 1, GLA_DK), lambda b, h, t: (h, 0, 0)),
            pl.BlockSpec((1, 1, GLA_DV), lambda b, h, t: (h, 0, 0)),
        ],
        out_specs=pl.BlockSpec((tt, GLA_DV), lambda b, h, t: (row(b, h, t), h)),
        out_shape=jax.ShapeDtypeStruct((n, GLA_HEADS * GLA_DV), BF16),
        scratch_shapes=[pltpu.VMEM((GLA_DV, GLA_DK), F32)],
        compiler_params=_params(("parallel", "parallel", "arbitrary"), 32),
        name="gla",
    )(proj, proj, proj, proj, proj, tri, wg, bg, go)


def _outproj_kernel(om_ref, og_ref, wa_ref, wb_ref, x_ref, g1_ref, sc_ref, sh_ref, gn_ref,
                    x1_ref, h2_ref):
    y = _dot(om_ref[...], wa_ref[...]) + _dot(og_ref[...], wb_ref[...])
    x1 = x_ref[...] + g1_ref[0] * y
    x1_ref[...] = x1
    h2_ref[...] = (_rms(x1, gn_ref[...]) * (1.0 + sc_ref[0]) + sh_ref[0]).astype(BF16)


def _outproj(om, og, wa, wb, x2d, g1, sc2, sh2, gn, seq):
    n, d = x2d.shape
    tm = 256
    bat = lambda i: (i * tm // seq, 0, 0)
    const = lambda i: (0, 0)
    return pl.pallas_call(
        _outproj_kernel,
        grid=(n // tm,),
        in_specs=[
            pl.BlockSpec((tm, om.shape[1]), lambda i: (i, 0)),
            pl.BlockSpec((tm, og.shape[1]), lambda i: (i, 0)),
            pl.BlockSpec(wa.shape, const),
            pl.BlockSpec(wb.shape, const),
            pl.BlockSpec((tm, d), lambda i: (i, 0)),
            pl.BlockSpec((1, 1, d), bat),
            pl.BlockSpec((1, 1, d), bat),
            pl.BlockSpec((1, 1, d), bat),
            pl.BlockSpec((1, d), const),
        ],
        out_specs=[
            pl.BlockSpec((tm, d), lambda i: (i, 0)),
            pl.BlockSpec((tm, d), lambda i: (i, 0)),
        ],
        out_shape=[
            jax.ShapeDtypeStruct((n, d), F32),
            jax.ShapeDtypeStruct((n, d), BF16),
        ],
        compiler_params=_params(("parallel",), 48),
        name="outproj",
    )(om, og, wa, wb, x2d, g1, sc2, sh2, gn)


def _top16(s, want_rank):
    cur = s
    rows = []
    rank = jnp.full(s.shape, float(PEER_TOPK), F32) if want_rank else None
    for k in range(PEER_TOPK):
        m = jnp.max(cur, axis=0, keepdims=True)
        rows.append(m)
        hit = cur == m
        if want_rank:
            rank = jnp.where(hit, float(k), rank)
        cur = jnp.where(hit, NEG, cur)
    return jnp.concatenate(rows, axis=0), rank


def _peertopk_kernel(h_ref, wq_ref, keys_ref, cnt_ref, w1_ref, r2_ref, e2_ref, q_scr):
    q = _dot(h_ref[...], wq_ref[...]).astype(BF16)
    for hp in range(2 * PEER_HEADS):
        q_scr[hp] = q[:, hp * LANE:(hp + 1) * LANE]
    tm = h_ref.shape[0]
    row8 = lax.broadcasted_iota(jnp.int32, (8, tm), 0)

    def head(h, carry):
        s1 = _dot_nt(keys_ref[2 * h], q_scr[2 * h])
        s2 = _dot_nt(keys_ref[2 * h + 1], q_scr[2 * h + 1])
        ts1, _ = _top16(s1, False)
        ts2, r2 = _top16(s2, True)
        cands = [ts1[0:1] + ts2]
        for k1 in range(1, 8):
            ck = ts1[k1:k1 + 1] + ts2[0:8]
            cands.append(jnp.where(row8 < PEER_TOPK // (k1 + 1), ck, NEG))
        cands.append(ts1[8:16] + ts2[0:1])
        cmax = ts1[0:1] + ts2[0:1]
        cur = list(cands)
        thr = cmax
        for k in range(PEER_TOPK):
            thr = functools.reduce(jnp.maximum, [jnp.max(c, axis=0, keepdims=True) for c in cur])
            cur = [jnp.where(c == thr, NEG, c) for c in cur]
        zsum = functools.reduce(
            jnp.add,
            [jnp.sum(jnp.where(c >= thr, jnp.exp(c - cmax), 0.0), axis=0, keepdims=True) for c in cands])
        cnt = jnp.zeros(s1.shape, F32)
        for k2 in range(PEER_TOPK):
            cnt = cnt + jnp.where(s1 + ts2[k2:k2 + 1] >= thr, 1.0, 0.0)
        cnt_ref[h] = cnt
        w1_ref[h] = jnp.exp(s1 - ts1[0:1]) / zsum
        r2_ref[h] = r2
        e2_ref[h] = jnp.exp(s2 - ts2[0:1])
        return carry

    lax.fori_loop(0, PEER_HEADS, head, 0)


def _peertopk(h2, wq, keys):
    n, d = h2.shape
    tm = 256
    shp = jax.ShapeDtypeStruct((PEER_HEADS, PEER_NKEYS, n), F32)
    ospec = pl.BlockSpec((PEER_HEADS, PEER_NKEYS, tm), lambda i: (0, 0, i))
    return pl.pallas_call(
        _peertopk_kernel,
        grid=(n // tm,),
        in_specs=[
            pl.BlockSpec((tm, d), lambda i: (i, 0)),
            pl.BlockSpec(wq.shape, lambda i: (0, 0)),
            pl.BlockSpec(keys.shape, lambda i: (0, 0, 0)),
        ],
        out_specs=[ospec] * 4,
        out_shape=[shp] * 4,
        scratch_shapes=[pltpu.VMEM((2 * PEER_HEADS, tm, LANE), BF16)],
        compiler_params=_params(("parallel",), 48),
        name="peertopk",
    )(h2, wq, keys)


def _peerdense_kernel(h_ref, u_ref, vt_ref, cnt_ref, w1_ref, r2_ref, e2_ref, x1_ref, g2_ref,
                      fg_ref, o_ref, acc_scr, p_scr, *, te, tm):
    j = pl.program_id(1)

    @pl.when(j == 0)
    def _():
        acc_scr[...] = jnp.zeros(acc_scr.shape, F32)

    act = _dot_nt(u_ref[...], h_ref[...])
    act = 0.5 * act * (1.0 + lax.erf(act * math.sqrt(0.5)))
    for g in range(te // PEER_NKEYS):
        rows = slice(g * PEER_NKEYS, (g + 1) * PEER_NKEYS)
        for lc in range(tm // LANE):
            cols = slice(lc * LANE, (lc + 1) * LANE)
            gate = jnp.zeros((PEER_NKEYS, LANE), F32)
            for h in range(PEER_HEADS):
                cb = cnt_ref[h, 0, g:g + 1, cols]
                wb = w1_ref[h, 0, g:g + 1, cols]
                gate = gate + jnp.where(r2_ref[h, :, cols] < cb, e2_ref[h, :, cols] * wb, 0.0)
            p_scr[rows, cols] = (act[rows, cols] * gate).astype(BF16)
    acc_scr[...] += _dot(vt_ref[...], p_scr[...])

    @pl.when(j == pl.num_programs(1) - 1)
    def _():
        x2 = x1_ref[...] + g2_ref[0] * acc_scr[...].T
        o_ref[...] = _rms(x2, fg_ref[...])


def _peerdense(h2, u, vt, cnt, w1, r2, e2, x1, g2, fg, seq):
    n, d = h2.shape
    ne = u.shape[0]
    tm = min(512, seq)
    te = 512
    gpb = te // PEER_NKEYS
    cnt = cnt.reshape(PEER_HEADS, PEER_NKEYS // gpb, gpb, n)
    w1 = w1.reshape(PEER_HEADS, PEER_NKEYS // gpb, gpb, n)
    return pl.pallas_call(
        functools.partial(_peerdense_kernel, te=te, tm=tm),
        grid=(n // tm, ne // te),
        in_specs=[
            pl.BlockSpec((tm, d), lambda i, j: (i, 0)),
            pl.BlockSpec((te, d), lambda i, j: (j, 0)),
            pl.BlockSpec((d, te), lambda i, j: (0, j)),
            pl.BlockSpec((PEER_HEADS, 1, gpb, tm), lambda i, j: (0, j, 0, i)),
            pl.BlockSpec((PEER_HEADS, 1, gpb, tm), lambda i, j: (0, j, 0, i)),
            pl.BlockSpec((PEER_HEADS, PEER_NKEYS, tm), lambda i, j: (0, 0, i)),
            pl.BlockSpec((PEER_HEADS, PEER_NKEYS, tm), lambda i, j: (0, 0, i)),
            pl.BlockSpec((tm, d), lambda i, j: (i, 0)),
            pl.BlockSpec((1, 1, d), lambda i, j: (i * tm // seq, 0, 0)),
            pl.BlockSpec((1, d), lambda i, j: (0, 0)),
        ],
        out_specs=pl.BlockSpec((tm, d), lambda i, j: (i, 0)),
        out_shape=jax.ShapeDtypeStruct((n, d), F32),
        scratch_shapes=[pltpu.VMEM((d, tm), F32), pltpu.VMEM((te, tm), BF16)],
        compiler_params=_params(("parallel", "arbitrary"), 56),
        name="peerdense",
    )(h2, u, vt, cnt, w1, r2, e2, x1, g2, fg)


def _rot_cols(w):
    half = w.shape[-1] // 2
    return jnp.concatenate([-w[..., half:], w[..., :half]], axis=-1)


def _prep_weights(w_in, w_uq, w_ukv, w_gate2, w_out, w_q, sub_keys, u_tab, v_tab):
    d = w_in.shape[0]
    o = [0]
    for s in (MLA_Q_RANK, MLA_KV_RANK, MLA_ROPE, GLA_HEADS * GLA_DK, GLA_HEADS * GLA_DK,
              GLA_HEADS * GLA_DV, GLA_GATE_RANK, GLA_HEADS * GLA_DV):
        o.append(o[-1] + s)
    cq, ckv, kpe, gq, gk, gv, glr, gr = [w_in[:, o[i]:o[i + 1]] for i in range(8)]
    z = lambda k: jnp.zeros((d, k), w_in.dtype)
    used = OFF_GLR + GLA_GATE_RANK
    w_in_r = jnp.concatenate(
        [cq, ckv, gq, gk, gv, gr, kpe, z(LANE - MLA_ROPE), _rot_cols(kpe), z(LANE - MLA_ROPE),
         glr, z(IN_COLS - used)], axis=1).astype(BF16)

    wq3 = w_uq.reshape(MLA_Q_RANK, MLA_HEADS, MLA_QK)
    zq = jnp.zeros((MLA_Q_RANK, MLA_HEADS, 2 * LANE - MLA_QK), w_uq.dtype)
    wq = jnp.concatenate([wq3, zq], axis=-1).reshape(MLA_Q_RANK, MLA_HEADS * 2 * LANE).astype(BF16)
    wqr = jnp.concatenate([_rot_cols(wq3[..., MLA_NOPE:]), zq], axis=-1)
    wqr = wqr.reshape(MLA_Q_RANK, MLA_HEADS * LANE).astype(BF16)
    wkv = w_ukv.astype(BF16)

    wg = w_gate2.reshape(GLA_GATE_RANK, GLA_HEADS, GLA_DK).transpose(1, 0, 2)
    wg = jnp.pad(wg, ((0, 0), (0, LANE - GLA_GATE_RANK), (0, 0))).astype(BF16)

    half = w_out.shape[0] // 2
    wa = w_out[:half].astype(BF16)
    wb = w_out[half:].astype(BF16)
    keys = sub_keys.reshape(PEER_HEADS * 2, PEER_NKEYS, sub_keys.shape[-1]).astype(BF16)
    return (w_in_r, wq, wqr, wkv, wg, wa, wb, w_q.astype(BF16), keys,
            u_tab.astype(BF16), v_tab.astype(BF16).T)


def kernel(x, c, positions, ada_w, ada_b, mix_norm_g, w_in, mla_q_norm_g, mla_w_uq, mla_kv_norm_g,
           mla_w_ukv, mla_out_norm_g, gla_w_gate2, gla_b_gate, gla_out_norm_g, w_out, ffn_norm_g,
           peer_w_q, peer_sub_keys, peer_u, peer_v, final_norm_g):
    batch, seq, d = x.shape
    n = batch * seq
    (w_in_r, wq, wqr, wkv, wg, wa, wb, wpq, keys, u_bf, vt_bf) = _prep_weights(
        w_in[0], mla_w_uq[0], mla_w_ukv[0], gla_w_gate2[0], w_out[0], peer_w_q[0],
        peer_sub_keys[0], peer_u[0], peer_v[0])

    c8 = jnp.pad(c, ((0, 8 - batch), (0, 0)))
    mod = _mod(c8, ada_w[0], ada_b[0].reshape(1, -1))[:batch]
    sh1, sc1, g1, sh2, sc2, g2 = [m.reshape(batch, 1, d) for m in jnp.split(mod, 6, axis=-1)]

    x2d = x.reshape(n, d)
    proj = _inproj(x2d, sc1, sh1, mix_norm_g[0].reshape(1, d), w_in_r, seq)

    inv_freq = ROPE_THETA ** (-jnp.arange(0, MLA_ROPE, 2, dtype=F32) / MLA_ROPE)
    frq = jnp.concatenate([inv_freq, inv_freq, jnp.zeros((LANE - MLA_ROPE,), F32)]).reshape(1, LANE)
    pos = positions.astype(F32).reshape(n, 1)
    q, kv, kpe = _mlaprep(proj, pos, frq, mla_q_norm_g[0].reshape(1, -1),
                          mla_kv_norm_g[0].reshape(1, -1), wq, wqr, wkv)
    o_mla = _attn(q, kv, kpe, mla_out_norm_g[0].reshape(MLA_HEADS, 1, MLA_V), batch, seq)

    tt = min(512, seq)
    ti = jnp.arange(tt)
    tri = ((ti[:, None] >= ti[None, :]) & (ti[:, None] // GLA_CHUNK == ti[None, :] // GLA_CHUNK))
    o_gla = _gla(proj, tri.astype(BF16), wg, gla_b_gate[0].reshape(GLA_HEADS, 1, GLA_DK),
                 gla_out_norm_g[0].reshape(GLA_HEADS, 1, GLA_DV), batch, seq)

    x1, h2 = _outproj(o_mla, o_gla, wa, wb, x2d, g1, sc2, sh2, ffn_norm_g[0].reshape(1, d), seq)
    cnt, w1, r2, e2 = _peertopk(h2, wpq, keys)
    y = _peerdense(h2, u_bf, vt_bf, cnt, w1, r2, e2, x1, g2, final_norm_g.reshape(1, d), seq)
    return y.reshape(batch, seq, d)
```
